```python
import math
import jax, jax.numpy as jnp
from jax import lax
import numpy as np

D_MODEL = 1024
BATCH = 2
SEQ = 16384
DEPTH = 1

PLE_DIM = 256
EPS = 1e-6

ATTN_HEADS = 8
ATTN_HEAD_DIM = 64
ATTN_WIDTH = ATTN_HEADS * ATTN_HEAD_DIM
DILATED_PAIRS = ((128, 1), (512, 4), (2048, 16))
ATTN_BLOCK = 128

MLSTM_HEADS = 4
MLSTM_HEAD_DIM = 128
MLSTM_WIDTH = MLSTM_HEADS * MLSTM_HEAD_DIM
MLSTM_CHUNK = 64
CONV_WIDTH = 4

PEER_HEADS = 8
PEER_N_KEYS = 128
PEER_N_EXPERTS = PEER_N_KEYS * PEER_N_KEYS
PEER_QUERY_DIM = 256
PEER_HALF_DIM = PEER_QUERY_DIM // 2
PEER_TOPK = 16
PEER_TOKEN_BLOCK = 128

N_IN = 3 * ATTN_WIDTH + 4 * MLSTM_WIDTH + 2 * MLSTM_HEADS + 2 * D_MODEL
IN_SPLITS = (
    ATTN_WIDTH,
    2 * ATTN_WIDTH,
    3 * ATTN_WIDTH,
    3 * ATTN_WIDTH + 2 * MLSTM_WIDTH,
    3 * ATTN_WIDTH + 3 * MLSTM_WIDTH,
    3 * ATTN_WIDTH + 4 * MLSTM_WIDTH,
    3 * ATTN_WIDTH + 4 * MLSTM_WIDTH + 2 * MLSTM_HEADS,
)

kernel_name = "hybrid_dilated_mlstm_peer_block"


def _rmsnorm(x, g):
    xf = x.astype(jnp.float32)
    y = xf * lax.rsqrt(jnp.mean(xf * xf, axis=-1, keepdims=True) + EPS) * g.astype(jnp.float32)
    return y.astype(x.dtype)


def _dilated_window_attention(q, k, v, window, dilation):
    b, s, h, e = q.shape
    w_sub = window // dilation
    sub_len = s // dilation
    nb = -(-sub_len // ATTN_BLOCK)
    lp = nb * ATTN_BLOCK
    scale = 1.0 / math.sqrt(e)

    def to_sub(t):
        t = t.reshape(b, sub_len, dilation, h, e).transpose(0, 2, 3, 1, 4)
        return jnp.pad(t, ((0, 0), (0, 0), (0, 0), (0, lp - sub_len), (0, 0)))

    def with_prev(t):
        tb = jnp.pad(t, ((0, 0), (0, 0), (0, 0), (ATTN_BLOCK, 0), (0, 0)))
        tb = tb.reshape(b, dilation, h, nb + 1, ATTN_BLOCK, e)
        return jnp.concatenate([tb[:, :, :, :-1], tb[:, :, :, 1:]], axis=4)

    qb = to_sub(q).reshape(b, dilation, h, nb, ATTN_BLOCK, e)
    kb = with_prev(to_sub(k))
    vb = with_prev(to_sub(v))

    scores = jnp.einsum('bdhnqe,bdhnke->bdhnqk', qb, kb).astype(jnp.float32) * scale
    blk = jnp.arange(nb)[:, None, None] * ATTN_BLOCK
    q_pos = blk + jnp.arange(ATTN_BLOCK)[None, :, None]
    k_pos = blk - ATTN_BLOCK + jnp.arange(2 * ATTN_BLOCK)[None, None, :]
    dist = q_pos - k_pos
    mask = (dist >= 0) & (dist <= w_sub) & (k_pos >= 0)
    scores = jnp.where(mask, scores, -jnp.inf)
    m = jnp.max(scores, axis=-1, keepdims=True)
    pr = jnp.exp(scores - m)
    denom = jnp.sum(pr, axis=-1, keepdims=True)
    out = jnp.einsum('bdhnqk,bdhnke->bdhnqe', (pr / denom).astype(v.dtype), vb).astype(jnp.float32)
    lse = (m + jnp.log(denom))[..., 0]

    out = out.reshape(b, dilation, h, lp, e)[:, :, :, :sub_len]
    out = out.transpose(0, 3, 1, 2, 4).reshape(b, s, h, e)
    lse = lse.reshape(b, dilation, h, lp)[:, :, :, :sub_len]
    lse = lse.transpose(0, 3, 1, 2).reshape(b, s, h)
    return out, lse


def _causal_depthwise_conv(x, w):
    c = x.shape[-1]
    return lax.conv_general_dilated(
        x, w.astype(x.dtype)[:, None, :], window_strides=(1,),
        padding=((CONV_WIDTH - 1, 0),), dimension_numbers=('NWC', 'WIO', 'NWC'),
        feature_group_count=c)


def _mlstm_chunkwise(q, k, v, i_pre, log_f):
    b, s, h, e = q.shape
    nc = s // MLSTM_CHUNK

    def chunks(t):
        t = t.astype(jnp.float32).reshape(b, nc, MLSTM_CHUNK, h, *t.shape[3:])
        return jnp.moveaxis(t, (1, 3), (0, 2))

    qc, kc, vc = chunks(q), chunks(k), chunks(v)
    ic, fc = chunks(i_pre), chunks(log_f)
    causal = jnp.tril(jnp.ones((MLSTM_CHUNK, MLSTM_CHUNK), dtype=bool))

    def step(carry, inp):
        c_st, n_st, m_st = carry
        qj, kj, vj, ij, fj = inp
        bcum = jnp.cumsum(fj, axis=-1)
        g = bcum[..., -1]
        d_intra = bcum[..., :, None] - bcum[..., None, :] + ij[..., None, :]
        d_intra = jnp.where(causal, d_intra, -jnp.inf)
        d_inter = bcum + m_st[..., None]
        m_t = jnp.maximum(d_inter, jnp.max(d_intra, axis=-1))
        w_intra = jnp.exp(d_intra - m_t[..., None])
        w_inter = jnp.exp(d_inter - m_t)
        qk = jnp.einsum('bhte,bhse->bhts', qj, kj) * w_intra
        num = (w_inter[..., None] * jnp.einsum('bhte,bhef->bhtf', qj, c_st)
               + jnp.einsum('bhts,bhsf->bhtf', qk, vj))
        den = w_inter * jnp.einsum('bhte,bhe->bht', qj, n_st) + jnp.sum(qk, axis=-1)
        h_out = num / jnp.maximum(jnp.abs(den), jnp.exp(-m_t))[..., None]
        d_state = g[..., None] - bcum + ij
        m_new = jnp.maximum(g + m_st, jnp.max(d_state, axis=-1))
        w_old = jnp.exp(g + m_st - m_new)
        w_s = jnp.exp(d_state - m_new[..., None])
        c_new = w_old[..., None, None] * c_st + jnp.einsum('bhs,bhse,bhsf->bhef', w_s, kj, vj)
        n_new = w_old[..., None] * n_st + jnp.einsum('bhs,bhse->bhe', w_s, kj)
        return (c_new, n_new, m_new), h_out

    init = (jnp.zeros((b, h, e, e), jnp.float32), jnp.zeros((b, h, e), jnp.float32),
            jnp.zeros((b, h), jnp.float32))
    _, hs = lax.scan(step, init, (qc, kc, vc, ic, fc))
    return jnp.moveaxis(hs, (0, 2), (1, 3)).reshape(b, s, h, e)


def _peer(h, w_query, keys1, keys2, expert_u, expert_v):
    b, s, d = h.shape
    q = (h @ w_query).reshape(b, s, PEER_HEADS, 2, PEER_HALF_DIM)
    s1 = jnp.einsum('bshe,hne->bshn', q[..., 0, :], keys1).astype(jnp.float32)
    s2 = jnp.einsum('bshe,hne->bshn', q[..., 1, :], keys2).astype(jnp.float32)
    v1, i1 = lax.top_k(s1, PEER_TOPK)
    v2, i2 = lax.top_k(s2, PEER_TOPK)
    cand_score = (v1[..., :, None] + v2[..., None, :]).reshape(b, s, PEER_HEADS, PEER_TOPK * PEER_TOPK)
    cand_idx = (i1[..., :, None] * PEER_N_KEYS + i2[..., None, :]).reshape(b, s, PEER_HEADS, PEER_TOPK * PEER_TOPK)
    top_score, top_pos = lax.top_k(cand_score, PEER_TOPK)
    idx = jnp.take_along_axis(cand_idx, top_pos, axis=-1)
    gates = jax.nn.softmax(top_score, axis=-1).astype(h.dtype)

    n_blk = (b * s) // PEER_TOKEN_BLOCK
    xt = h.reshape(n_blk, PEER_TOKEN_BLOCK, d)
    it = idx.reshape(n_blk, PEER_TOKEN_BLOCK, PEER_HEADS, PEER_TOPK)
    gt = gates.reshape(n_blk, PEER_TOKEN_BLOCK, PEER_HEADS, PEER_TOPK)

    def block(args):
        xb, ib, gb = args
        u_sel = jnp.take(expert_u, ib, axis=0)
        act = jax.nn.gelu(jnp.einsum('thkd,td->thk', u_sel, xb), approximate=False)
        v_sel = jnp.take(expert_v, ib, axis=0)
        return jnp.einsum('thk,thkd->td', gb * act, v_sel)

    y = lax.map(block, (xt, it, gt))
    return y.reshape(b, s, d)


def setup_inputs(seed: int = 0) -> dict:
    key = jax.random.key(seed)
    ks = jax.random.split(key, 24)
    f32 = jnp.float32

    def nrm(k, shape, scale):
        return jax.random.normal(k, shape, f32) * scale

    def gain(k, shape):
        return 1.0 + 0.05 * jax.random.normal(k, shape, f32)

    i_bias = 0.1 * jax.random.normal(ks[3], (DEPTH, MLSTM_HEADS), f32)
    f_bias = jnp.linspace(3.0, 6.0, MLSTM_HEADS, dtype=f32)[None, :] + 0.01 * jax.random.normal(ks[4], (DEPTH, MLSTM_HEADS), f32)
    return {
        "x": nrm(ks[0], (BATCH, SEQ, D_MODEL), 1.0),
        "p": nrm(ks[1], (DEPTH, BATCH, SEQ, PLE_DIM), 1.0),
        "norm_mix": gain(ks[2], (DEPTH, D_MODEL)),
        "w_in": nrm(ks[5], (DEPTH, D_MODEL, N_IN), D_MODEL ** -0.5),
        "conv_qk": nrm(ks[6], (DEPTH, CONV_WIDTH, 2 * MLSTM_WIDTH), CONV_WIDTH ** -0.5),
        "gate_bias": jnp.concatenate([i_bias, f_bias], axis=-1),
        "mlstm_norm": gain(ks[7], (DEPTH, MLSTM_WIDTH)),
        "w_up_attn": nrm(ks[8], (DEPTH, ATTN_WIDTH, D_MODEL), ATTN_WIDTH ** -0.5),
        "w_up_mlstm": nrm(ks[9], (DEPTH, MLSTM_WIDTH, D_MODEL), MLSTM_WIDTH ** -0.5),
        "w_out": nrm(ks[10], (DEPTH, D_MODEL, D_MODEL), D_MODEL ** -0.5),
        "norm_ffn": gain(ks[11], (DEPTH, D_MODEL)),
        "w_query": nrm(ks[12], (DEPTH, D_MODEL, PEER_HEADS * PEER_QUERY_DIM), D_MODEL ** -0.5),
        "keys1": nrm(ks[13], (DEPTH, PEER_HEADS, PEER_N_KEYS, PEER_HALF_DIM), PEER_HALF_DIM ** -0.5),
        "keys2": nrm(ks[14], (DEPTH, PEER_HEADS, PEER_N_KEYS, PEER_HALF_DIM), PEER_HALF_DIM ** -0.5),
        "expert_u": nrm(ks[15], (DEPTH, PEER_N_EXPERTS, D_MODEL), D_MODEL ** -0.5),
        "expert_v": nrm(ks[16], (DEPTH, PEER_N_EXPERTS, D_MODEL), PEER_HEADS ** -0.5),
        "norm_ple": gain(ks[17], (DEPTH, D_MODEL)),
        "w_ple_gate": nrm(ks[18], (DEPTH, D_MODEL, D_MODEL), D_MODEL ** -0.5),
        "w_ple": nrm(ks[19], (DEPTH, PLE_DIM, D_MODEL), PLE_DIM ** -0.5),
        "norm_final": gain(ks[20], (D_MODEL,)),
    }


def reference(x, p, norm_mix, w_in, conv_qk, gate_bias, mlstm_norm, w_up_attn, w_up_mlstm,
              w_out, norm_ffn, w_query, keys1, keys2, expert_u, expert_v, norm_ple,
              w_ple_gate, w_ple, norm_final):
    b, s, _ = x.shape
    for layer in range(DEPTH):
        h = _rmsnorm(x, norm_mix[layer])
        proj = h @ w_in[layer]
        q_a, k_a, v_a, qk_m, v_m, o_m, if_m, gate_pre = jnp.split(proj, IN_SPLITS, axis=-1)

        qa = q_a.reshape(b, s, ATTN_HEADS, ATTN_HEAD_DIM)
        ka = k_a.reshape(b, s, ATTN_HEADS, ATTN_HEAD_DIM)
        va = v_a.reshape(b, s, ATTN_HEADS, ATTN_HEAD_DIM)
        outs, lses = [], []
        for window, dilation in DILATED_PAIRS:
            o_g, lse_g = _dilated_window_attention(qa, ka, va, window, dilation)
            outs.append(o_g)
            lses.append(lse_g)
        mix_w = jax.nn.softmax(jnp.stack(lses, axis=0), axis=0)
        y_a = jnp.sum(mix_w[..., None] * jnp.stack(outs, axis=0), axis=0)
        y_a = y_a.reshape(b, s, ATTN_WIDTH).astype(x.dtype)

        qk_c = jax.nn.silu(_causal_depthwise_conv(qk_m, conv_qk[layer]))
        q_m, k_m = jnp.split(qk_c, 2, axis=-1)
        q_m = q_m.reshape(b, s, MLSTM_HEADS, MLSTM_HEAD_DIM)
        k_m = k_m.reshape(b, s, MLSTM_HEADS, MLSTM_HEAD_DIM) * (MLSTM_HEAD_DIM ** -0.5)
        vm = v_m.reshape(b, s, MLSTM_HEADS, MLSTM_HEAD_DIM)
        gif = if_m.astype(jnp.float32) + gate_bias[layer].astype(jnp.float32)
        i_pre, f_pre = jnp.split(gif, 2, axis=-1)
        h_m = _mlstm_chunkwise(q_m, k_m, vm, i_pre, jax.nn.log_sigmoid(f_pre))
        h_m = h_m * lax.rsqrt(jnp.mean(h_m * h_m, axis=-1, keepdims=True) + EPS)
        y_m = (h_m.reshape(b, s, MLSTM_WIDTH) * mlstm_norm[layer].astype(jnp.float32)).astype(x.dtype)
        y_m = y_m * jax.nn.sigmoid(o_m)

        g_a, g_m = jnp.split(gate_pre, 2, axis=-1)
        merged = (jax.nn.sigmoid(g_a) * (y_a @ w_up_attn[layer])
                  + jax.nn.sigmoid(g_m) * (y_m @ w_up_mlstm[layer]))
        x = x + merged @ w_out[layer]

        x = x + _peer(_rmsnorm(x, norm_ffn[layer]), w_query[layer], keys1[layer], keys2[layer],
                      expert_u[layer], expert_v[layer])

        ple_gate = jax.nn.sigmoid(_rmsnorm(x, norm_ple[layer]) @ w_ple_gate[layer])
        x = x + ple_gate * (p[layer] @ w_ple[layer])
    return _rmsnorm(x, norm_final)
```

```python
import functools
import math

import jax
import jax.numpy as jnp
from jax import lax
from jax.experimental import pallas as pl
from jax.experimental.pallas import tpu as pltpu

F32 = jnp.float32
BF16 = jnp.bfloat16

EPS = 1e-6
LANES = 128
ATTN_HEADS = 8
ATTN_HEAD_DIM = 64
ATTN_BLOCK = 128
DILATED_PAIRS = ((128, 1), (512, 4), (2048, 16))
MLSTM_HEADS = 4
MLSTM_HEAD_DIM = 128
CONV_WIDTH = 4
PEER_HEADS = 8
PEER_N_KEYS = 128
PEER_TOPK = 16
VMEM_LIMIT = 56 * 1024 * 1024


def _params(sem, vmem=VMEM_LIMIT):
    return pltpu.CompilerParams(dimension_semantics=sem, vmem_limit_bytes=vmem)


def _full(shape):
    nd = len(shape)
    return pl.BlockSpec(shape, lambda *_: (0,) * nd)


def _rms(x, g):
    return x * lax.rsqrt(jnp.mean(x * x, axis=-1, keepdims=True) + EPS) * g


def _in_proj_kernel(x_ref, g_ref, *refs):
    n = len(refs) // 2
    w_refs, o_refs = refs[:n], refs[n:]
    h = _rms(x_ref[...], g_ref[...]).astype(BF16)
    for w_ref, o_ref in zip(w_refs, o_refs):
        o_ref[...] = jnp.dot(h, w_ref[...], preferred_element_type=F32).astype(o_ref.dtype)


def _in_proj(x2, g, weights, out_dtypes, tm=256):
    n_tok, d = x2.shape
    return pl.pallas_call(
        _in_proj_kernel,
        out_shape=[jax.ShapeDtypeStruct((n_tok, w.shape[1]), dt) for w, dt in zip(weights, out_dtypes)],
        grid=(n_tok // tm,),
        in_specs=[pl.BlockSpec((tm, d), lambda i: (i, 0)), _full((1, d))]
        + [_full(w.shape) for w in weights],
        out_specs=[pl.BlockSpec((tm, w.shape[1]), lambda i: (i, 0)) for w in weights],
        compiler_params=_params(("parallel",)),
        name="in_proj",
    )(x2, g.reshape(1, d), *weights)


def _attn_kernel(q_ref, kp_ref, kc_ref, vp_ref, vc_ref, o_ref, l_ref):
    blk = ATTN_BLOCK
    n = pl.program_id(2)
    scale = 1.0 / math.sqrt(ATTN_HEAD_DIM)
    row = lax.broadcasted_iota(jnp.int32, (blk, blk), 0)
    col = lax.broadcasted_iota(jnp.int32, (blk, blk), 1)
    keep_prev = col >= row
    keep_cur = col <= row
    first = jnp.where(n > 0, 0.0, -jnp.inf).astype(F32)
    lo = col < ATTN_HEAD_DIM
    lane_lo = (lax.broadcasted_iota(jnp.int32, (1, LANES), 1) < ATTN_HEAD_DIM)
    half_masks = (lane_lo.astype(BF16), (~lane_lo).astype(BF16))
    nt = (((1,), (1,)), ((), ()))
    for pair in range(ATTN_HEADS // 2):
        sl = slice(pair * LANES, (pair + 1) * LANES)
        q2 = q_ref[:, sl]
        kp, kc, vp, vc = kp_ref[:, sl], kc_ref[:, sl], vp_ref[:, sl], vc_ref[:, sl]
        res = []
        for half in range(2):
            qh = q2 * half_masks[half]
            sp = lax.dot_general(qh, kp, nt, preferred_element_type=F32) * scale
            sc = lax.dot_general(qh, kc, nt, preferred_element_type=F32) * scale
            sp = jnp.where(keep_prev, sp + first, -jnp.inf)
            sc = jnp.where(keep_cur, sc, -jnp.inf)
            m = jnp.maximum(jnp.max(sp, axis=-1, keepdims=True), jnp.max(sc, axis=-1, keepdims=True))
            pp = jnp.exp(sp - m)
            pc = jnp.exp(sc - m)
            den = jnp.sum(pp, axis=-1, keepdims=True) + jnp.sum(pc, axis=-1, keepdims=True)
            o = jnp.dot(pp.astype(BF16), vp, preferred_element_type=F32)
            o = o + jnp.dot(pc.astype(BF16), vc, preferred_element_type=F32)
            res.append((o / den, m + jnp.log(den)))
        o_ref[:, sl] = jnp.where(lo, res[0][0], res[1][0])
        l_ref[:, sl] = jnp.where(lo, res[0][1], res[1][1])


def _attn(qkv, dilation):
    b, s, w3 = qkv.shape
    w = w3 // 3
    sub = s // dilation
    nb = sub // ATTN_BLOCK
    view = qkv.reshape(b, sub, dilation * w3)
    blk = (None, ATTN_BLOCK, w)

    def spec(part, prev):
        if prev:
            return pl.BlockSpec(blk, lambda bi, r, n: (bi, jnp.maximum(n - 1, 0), 3 * r + part))
        return pl.BlockSpec(blk, lambda bi, r, n: (bi, n, 3 * r + part))

    out_spec = pl.BlockSpec(blk, lambda bi, r, n: (bi, n, r))
    shape = jax.ShapeDtypeStruct((b, sub, dilation * w), F32)
    o, l = pl.pallas_call(
        _attn_kernel,
        out_shape=[shape, shape],
        grid=(b, dilation, nb),
        in_specs=[spec(0, False), spec(1, True), spec(1, False), spec(2, True), spec(2, False)],
        out_specs=[out_spec, out_spec],
        compiler_params=_params(("parallel", "parallel", "arbitrary")),
        name=f"attn_d{dilation}",
    )(view, view, view, view, view)
    return o.reshape(b * s, w), l.reshape(b * s, w)


MLSTM_CHUNK = 128
_HIGHEST = lax.Precision.HIGHEST


def _log_sigmoid(x):
    return jnp.minimum(x, 0.0) - jnp.log1p(jnp.exp(-jnp.abs(x)))


def _mlstm_kernel(qk_ref, v_ref, og_ref, gif_ref, cw_ref, gb_ref, nw_ref, y_ref,
                  xs_ref, c_ref, n_ref, m_ref):
    t_tile = qk_ref.shape[0]
    width = v_ref.shape[1]
    ln = MLSTM_CHUNK
    e = MLSTM_HEAD_DIM
    pad = 8

    @pl.when(pl.program_id(1) == 0)
    def _():
        xs_ref[0:pad, :] = jnp.zeros((pad, 2 * width), F32)
        c_ref[...] = jnp.zeros_like(c_ref)
        n_ref[...] = jnp.zeros_like(n_ref)
        m_ref[...] = jnp.zeros_like(m_ref)

    xs_ref[pad:pad + t_tile, :] = qk_ref[...]
    row = lax.broadcasted_iota(jnp.int32, (ln, ln), 0)
    col = lax.broadcasted_iota(jnp.int32, (ln, ln), 1)
    causal = col <= row
    tril = causal.astype(F32)
    triu = (row <= col).astype(F32)
    cw = cw_ref[...]
    tn = (((0,), (0,)), ((), ()))
    nt = (((1,), (1,)), ((), ()))

    for c in range(t_tile // ln):
        r0 = c * ln
        conv = cw[0:1, :] * xs_ref[pad - 3 + r0:pad - 3 + r0 + ln, :]
        for j in range(1, CONV_WIDTH):
            conv = conv + cw[j:j + 1, :] * xs_ref[pad - 3 + j + r0:pad - 3 + j + r0 + ln, :]
        qk = conv * jax.nn.sigmoid(conv)
        gates = gif_ref[r0:r0 + ln, :] + gb_ref[...]
        lsg = _log_sigmoid(gates)
        gates_t = gates.T
        bcum_col = jnp.dot(tril, lsg, precision=_HIGHEST, preferred_element_type=F32)
        bcum_row = jnp.dot(lsg.T, triu, precision=_HIGHEST, preferred_element_type=F32)
        for h in range(MLSTM_HEADS):
            hs = slice(h * e, (h + 1) * e)
            q = qk[:, h * e:(h + 1) * e]
            k = qk[:, width + h * e:width + (h + 1) * e] * (e ** -0.5)
            qb = q.astype(BF16)
            v = v_ref[r0:r0 + ln, hs]
            fh = MLSTM_HEADS + h
            b_col = bcum_col[:, fh:fh + 1]
            b_row = bcum_row[fh:fh + 1, :]
            i_col = gates[:, h:h + 1]
            i_row = gates_t[h:h + 1, :]
            g_tot = b_col[ln - 1:ln, :]
            m_st = m_ref[h, 0:1, 0:1]
            n_st = n_ref[h, 0:1, :]
            c_st = c_ref[h]

            d_intra = jnp.where(causal, b_col - b_row + i_row, -jnp.inf)
            d_inter = b_col + m_st
            m_t = jnp.maximum(d_inter, jnp.max(d_intra, axis=-1, keepdims=True))
            w_intra = jnp.exp(d_intra - m_t)
            w_inter = jnp.exp(d_inter - m_t)
            s_qk = lax.dot_general(qb, k.astype(BF16), nt, preferred_element_type=F32) * w_intra
            num = (w_inter * jnp.dot(qb, c_st.astype(BF16), preferred_element_type=F32)
                   + jnp.dot(s_qk.astype(BF16), v, preferred_element_type=F32))
            den = (w_inter * jnp.sum(q * n_st, axis=-1, keepdims=True)
                   + jnp.sum(s_qk, axis=-1, keepdims=True))
            h_out = num / jnp.maximum(jnp.abs(den), jnp.exp(-m_t))

            d_state = g_tot - b_col + i_col
            m_new = jnp.maximum(g_tot + m_st, jnp.max(d_state, axis=0, keepdims=True))
            w_old = jnp.exp(g_tot + m_st - m_new)
            kw = k * jnp.exp(d_state - m_new)
            c_ref[h] = w_old * c_st + lax.dot_general(kw.astype(BF16), v, tn, preferred_element_type=F32)
            n_ref[h] = jnp.broadcast_to(w_old * n_st + jnp.sum(kw, axis=0, keepdims=True), (pad, e))
            m_ref[h] = jnp.broadcast_to(m_new, (pad, LANES))

            hn = h_out * lax.rsqrt(jnp.mean(h_out * h_out, axis=-1, keepdims=True) + EPS) * nw_ref[:, hs]
            y_ref[r0:r0 + ln, hs] = (hn * jax.nn.sigmoid(og_ref[r0:r0 + ln, hs])).astype(y_ref.dtype)

    xs_ref[0:pad, :] = xs_ref[t_tile:t_tile + pad, :]


def _mlstm(qk_m, v_m, o_m, gif, conv_w, gate_bias_pad, norm_w, batch, t_tile=512):
    n_tok, width2 = qk_m.shape
    width = width2 // 2
    seq = n_tok // batch
    nt = seq // t_tile

    def tok(cols):
        return pl.BlockSpec((t_tile, cols), lambda b, j: (b * nt + j, 0))

    return pl.pallas_call(
        _mlstm_kernel,
        out_shape=jax.ShapeDtypeStruct((n_tok, width), BF16),
        grid=(batch, nt),
        in_specs=[tok(width2), tok(width), tok(width), tok(LANES),
                  _full(conv_w.shape), _full((1, LANES)), _full((1, width))],
        out_specs=tok(width),
        scratch_shapes=[
            pltpu.VMEM((t_tile + 8, width2), F32),
            pltpu.VMEM((MLSTM_HEADS, MLSTM_HEAD_DIM, MLSTM_HEAD_DIM), F32),
            pltpu.VMEM((MLSTM_HEADS, 8, MLSTM_HEAD_DIM), F32),
            pltpu.VMEM((MLSTM_HEADS, 8, LANES), F32),
        ],
        compiler_params=_params(("parallel", "arbitrary")),
        name="mlstm",
    )(qk_m, v_m, o_m, gif, conv_w, gate_bias_pad, norm_w.reshape(1, width))


def _merge_kernel(x_ref, o1_ref, l1_ref, o2_ref, l2_ref, o3_ref, l3_ref, ym_ref, gp_ref,
                  wa_ref, wm_ref, wo_ref, out_ref):
    d = x_ref.shape[1]
    lses = (l1_ref[...], l2_ref[...], l3_ref[...])
    outs = (o1_ref[...], o2_ref[...], o3_ref[...])
    top = jnp.maximum(jnp.maximum(lses[0], lses[1]), lses[2])
    ws = [jnp.exp(l - top) for l in lses]
    y_a = (ws[0] * outs[0] + ws[1] * outs[1] + ws[2] * outs[2]) / (ws[0] + ws[1] + ws[2])
    up_a = jnp.dot(y_a.astype(BF16), wa_ref[...], preferred_element_type=F32)
    up_m = jnp.dot(ym_ref[...], wm_ref[...], preferred_element_type=F32)
    merged = jax.nn.sigmoid(gp_ref[:, 0:d]) * up_a + jax.nn.sigmoid(gp_ref[:, d:2 * d]) * up_m
    out_ref[...] = x_ref[...] + jnp.dot(merged.astype(BF16), wo_ref[...], preferred_element_type=F32)


def _merge(x2, attn_outs, y_m, gate_pre, w_up_a, w_up_m, w_out, tm=256):
    n_tok, d = x2.shape

    def tok(cols):
        return pl.BlockSpec((tm, cols), lambda i: (i, 0))

    flat = [a for pair in attn_outs for a in pair]
    return pl.pallas_call(
        _merge_kernel,
        out_shape=jax.ShapeDtypeStruct((n_tok, d), F32),
        grid=(n_tok // tm,),
        in_specs=[tok(d)] + [tok(a.shape[1]) for a in flat] + [tok(y_m.shape[1]), tok(2 * d),
                  _full(w_up_a.shape), _full(w_up_m.shape), _full(w_out.shape)],
        out_specs=tok(d),
        compiler_params=_params(("parallel",)),
        name="merge",
    )(x2, *flat, y_m, gate_pre, w_up_a, w_up_m, w_out)


def _candidates():
    k = PEER_TOPK
    return [(a, b) for a in range(k) for b in range(k) if (a + 1) * (b + 1) <= k]


def _top_ranks(s, lanef, rounds):
    work = s
    vals = jnp.zeros_like(s)
    rank = jnp.full(s.shape, float(LANES - 1), F32)
    top = None
    for a in range(rounds):
        m = jnp.max(work, axis=-1, keepdims=True)
        idx = jnp.min(jnp.where(work == m, lanef, float(LANES)), axis=-1, keepdims=True)
        hit = lanef == idx
        vals = jnp.where(lanef == float(a), m, vals)
        rank = jnp.where(hit, float(a), rank)
        work = jnp.where(hit, -jnp.inf, work)
        if a == 0:
            top = m
    return vals, rank, top


def _route_kernel(x_ref, g_ref, wq_ref, k1_ref, k2_ref, ca_ref, cb_ref, cnt_ref,
                  hnt_ref, a1t_ref, c1t_ref, a2t_ref, r2t_ref, hb_ref):
    t_tile = x_ref.shape[0]
    k = PEER_TOPK

    @pl.when(pl.program_id(1) == 0)
    def _():
        hn = _rms(x_ref[...], g_ref[...])
        hb_ref[...] = hn.astype(BF16)
        hnt_ref[...] = hn.T.astype(BF16)

    lanef = lax.broadcasted_iota(jnp.int32, (t_tile, LANES), 1).astype(F32)
    q = jnp.dot(hb_ref[...], wq_ref[...], preferred_element_type=F32).astype(BF16)
    s1 = jnp.dot(q[:, 0:LANES], k1_ref[0], preferred_element_type=F32)
    s2 = jnp.dot(q[:, LANES:2 * LANES], k2_ref[0], preferred_element_type=F32)
    v1, r1, top1 = _top_ranks(s1, lanef, k)
    v2, r2, top2 = _top_ranks(s2, lanef, k)

    shape = (t_tile, LANES)
    cand = (jnp.take_along_axis(v1, jnp.broadcast_to(ca_ref[...], shape), axis=1)
            + jnp.take_along_axis(v2, jnp.broadcast_to(cb_ref[...], shape), axis=1))
    cand = jnp.where(lanef < float(len(_candidates())), cand, -jnp.inf)
    _, crank, ctop = _top_ranks(cand, lanef, k)
    sel = crank < float(k)
    z = jnp.sum(jnp.where(sel, jnp.exp(cand - ctop), 0.0), axis=-1, keepdims=True)
    counts = jnp.dot(sel.astype(BF16), cnt_ref[...], preferred_element_type=F32)
    c1 = jnp.take_along_axis(counts, r1.astype(jnp.int32), axis=1)
    a1 = jnp.where(r1 < float(k), jnp.exp(s1 - top1), 0.0)
    a2 = jnp.where(r2 < float(k), jnp.exp(s2 - top2), 0.0) / z
    a1t_ref[...] = a1.T
    c1t_ref[...] = c1.T
    a2t_ref[...] = a2.T
    r2t_ref[...] = r2.T


def _route(x1, g, w_query, keys1_t, keys2_t, t_tile=256):
    n_tok, d = x1.shape
    cands = _candidates()
    ca = jnp.array([[a for a, _ in cands] + [0] * (LANES - len(cands))], jnp.int32)
    cb = jnp.array([[b for _, b in cands] + [0] * (LANES - len(cands))], jnp.int32)
    cnt = jnp.zeros((LANES, LANES), F32).at[jnp.arange(len(cands)), ca[0, :len(cands)]].set(1.0)
    hk = PEER_HEADS * PEER_N_KEYS
    qd = w_query.shape[1] // PEER_HEADS
    kt = pl.BlockSpec((1, LANES, PEER_N_KEYS), lambda i, h: (h, 0, 0))
    out_t = pl.BlockSpec((PEER_N_KEYS, t_tile), lambda i, h: (h, i))
    return pl.pallas_call(
        _route_kernel,
        out_shape=[jax.ShapeDtypeStruct((d, n_tok), BF16)] + [jax.ShapeDtypeStruct((hk, n_tok), F32)] * 4,
        grid=(n_tok // t_tile, PEER_HEADS),
        in_specs=[pl.BlockSpec((t_tile, d), lambda i, h: (i, 0)), _full((1, d)),
                  pl.BlockSpec((d, qd), lambda i, h: (0, h)), kt, kt,
                  _full((1, LANES)), _full((1, LANES)), _full((LANES, LANES))],
        out_specs=[pl.BlockSpec((d, t_tile), lambda i, h: (0, i))] + [out_t] * 4,
        scratch_shapes=[pltpu.VMEM((t_tile, d), BF16)],
        compiler_params=_params(("parallel", "arbitrary")),
        name="route",
    )(x1, g.reshape(1, d), w_query, keys1_t, keys2_t, ca, cb, cnt.astype(BF16))


def _experts_kernel(hnt_ref, a1t_ref, c1t_ref, a2t_ref, r2t_ref, u_ref, vt_ref, y_ref,
                    acc_ref, act_ref, gated_ref, a1b_ref, c1b_ref):
    e_blk = pl.program_id(1)
    n_i1 = u_ref.shape[0] // PEER_N_KEYS
    t_tile = hnt_ref.shape[1]
    rows = 16

    @pl.when(e_blk == 0)
    def _():
        acc_ref[...] = jnp.zeros_like(acc_ref)

    act_ref[...] = jnp.dot(u_ref[...], hnt_ref[...], preferred_element_type=F32)

    for h in range(PEER_HEADS):
        for ii in range(n_i1):
            a1b_ref[ii * PEER_HEADS + h] = jnp.broadcast_to(a1t_ref[h, ii:ii + 1, :], (rows, t_tile))
            c1b_ref[ii * PEER_HEADS + h] = jnp.broadcast_to(c1t_ref[h, ii:ii + 1, :], (rows, t_tile))

    def per_i1(ii, carry):
        base = pl.multiple_of(ii * PEER_N_KEYS, PEER_N_KEYS)
        for tl in range(t_tile // LANES):
            cols = slice(tl * LANES, (tl + 1) * LANES)
            c1 = [c1b_ref[ii * PEER_HEADS + h, :, cols] for h in range(PEER_HEADS)]
            a1 = [a1b_ref[ii * PEER_HEADS + h, :, cols] for h in range(PEER_HEADS)]
            for g in range(PEER_N_KEYS // rows):
                w = jnp.zeros((rows, LANES), F32)
                for h in range(PEER_HEADS):
                    r0 = h * PEER_N_KEYS + g * rows
                    hit = r2t_ref[r0:r0 + rows, cols] < c1[h]
                    w = w + jnp.where(hit, a2t_ref[r0:r0 + rows, cols], 0.0) * a1[h]
                a = act_ref[pl.ds(base + g * rows, rows), cols]
                gelu = 0.5 * a * (1.0 + lax.erf(a * (2.0 ** -0.5)))
                gated_ref[pl.ds(base + g * rows, rows), cols] = (w * gelu).astype(BF16)
        return carry

    lax.fori_loop(0, n_i1, per_i1, 0)
    acc_ref[...] += jnp.dot(vt_ref[...], gated_ref[...], preferred_element_type=F32)

    @pl.when(e_blk == pl.num_programs(1) - 1)
    def _():
        y_ref[...] = acc_ref[...].T


def _experts(hnt, a1t, c1t, a2t, r2t, u, vt, t_tile=512, e_tile=1024):
    d, n_tok = hnt.shape
    n_exp = u.shape[0]
    hk = a1t.shape[0]
    n_i1 = e_tile // PEER_N_KEYS
    fac2 = pl.BlockSpec((hk, t_tile), lambda i, e: (0, i))
    fac1 = pl.BlockSpec((PEER_HEADS, n_i1, t_tile), lambda i, e: (0, e, i))
    half1 = (PEER_HEADS, PEER_N_KEYS, n_tok)
    return pl.pallas_call(
        _experts_kernel,
        out_shape=jax.ShapeDtypeStruct((n_tok, d), F32),
        grid=(n_tok // t_tile, n_exp // e_tile),
        in_specs=[pl.BlockSpec((d, t_tile), lambda i, e: (0, i)), fac1, fac1, fac2, fac2,
                  pl.BlockSpec((e_tile, d), lambda i, e: (e, 0)),
                  pl.BlockSpec((d, e_tile), lambda i, e: (0, e))],
        out_specs=pl.BlockSpec((t_tile, d), lambda i, e: (i, 0)),
        scratch_shapes=[pltpu.VMEM((d, t_tile), F32), pltpu.VMEM((e_tile, t_tile), F32),
                        pltpu.VMEM((e_tile, t_tile), BF16),
                        pltpu.VMEM((n_i1 * PEER_HEADS, 16, t_tile), F32),
                        pltpu.VMEM((n_i1 * PEER_HEADS, 16, t_tile), F32)],
        compiler_params=_params(("parallel", "arbitrary")),
        name="experts",
    )(hnt, a1t.reshape(half1), c1t.reshape(half1), a2t, r2t, u, vt)


def _final_kernel(x_ref, y_ref, p_ref, gp_ref, wg_ref, wp_ref, gf_ref, out_ref, *, last):
    x = x_ref[...] + y_ref[...]
    gate = jax.nn.sigmoid(jnp.dot(_rms(x, gp_ref[...]).astype(BF16), wg_ref[...],
                                  preferred_element_type=F32))
    x = x + gate * jnp.dot(p_ref[...].astype(BF16), wp_ref[...], preferred_element_type=F32)
    out_ref[...] = _rms(x, gf_ref[...]) if last else x


def _final(x1, y, p2, g_ple, w_gate, w_ple, g_final, last, tm=256):
    n_tok, d = x1.shape

    def tok(cols):
        return pl.BlockSpec((tm, cols), lambda i: (i, 0))

    return pl.pallas_call(
        functools.partial(_final_kernel, last=last),
        out_shape=jax.ShapeDtypeStruct((n_tok, d), F32),
        grid=(n_tok // tm,),
        in_specs=[tok(d), tok(d), tok(p2.shape[1]), _full((1, d)), _full(w_gate.shape),
                  _full(w_ple.shape), _full((1, d))],
        out_specs=tok(d),
        compiler_params=_params(("parallel",)),
        name="final",
    )(x1, y, p2, g_ple.reshape(1, d), w_gate, w_ple, g_final.reshape(1, d))


def kernel(x, p, norm_mix, w_in, conv_qk, gate_bias, mlstm_norm, w_up_attn, w_up_mlstm, w_out,
           norm_ffn, w_query, keys1, keys2, expert_u, expert_v, norm_ple, w_ple_gate, w_ple,
           norm_final):
    b, s, d = x.shape
    depth = w_in.shape[0]
    aw = ATTN_HEADS * ATTN_HEAD_DIM
    mw = MLSTM_HEADS * MLSTM_HEAD_DIM
    for window, dilation in DILATED_PAIRS:
        assert window // dilation == ATTN_BLOCK and s % (dilation * ATTN_BLOCK) == 0
    x2 = x.reshape(b * s, d)
    for layer in range(depth):
        w = w_in[layer]
        c0 = 3 * aw
        c1 = c0 + 2 * mw
        c2 = c1 + mw
        c3 = c2 + mw
        c4 = c3 + 2 * MLSTM_HEADS
        w_if = jnp.pad(w[:, c3:c4], ((0, 0), (0, LANES - 2 * MLSTM_HEADS)))
        pieces = [w[:, :c0], w[:, c0:c1], w[:, c1:c2], w[:, c2:c3], w_if, w[:, c4:]]
        qkv_a, qk_m, v_m, o_m, gif, gate_pre = _in_proj(
            x2, norm_mix[layer], [q.astype(BF16) for q in pieces], [BF16, F32, BF16, F32, F32, F32])

        qkv3 = qkv_a.reshape(b, s, 3 * aw)
        attn_outs = [_attn(qkv3, dilation) for _, dilation in DILATED_PAIRS]

        bias_pad = jnp.pad(gate_bias[layer].astype(F32), (0, LANES - 2 * MLSTM_HEADS)).reshape(1, LANES)
        y_m = _mlstm(qk_m, v_m, o_m, gif, conv_qk[layer].astype(F32), bias_pad, mlstm_norm[layer], b)

        x1 = _merge(x2, attn_outs, y_m, gate_pre, w_up_attn[layer].astype(BF16),
                    w_up_mlstm[layer].astype(BF16), w_out[layer].astype(BF16))

        k1t = jnp.swapaxes(keys1[layer], 1, 2).astype(BF16)
        k2t = jnp.swapaxes(keys2[layer], 1, 2).astype(BF16)
        hnt, a1t, c1t, a2t, r2t = _route(x1, norm_ffn[layer], w_query[layer].astype(BF16), k1t, k2t)
        y = _experts(hnt, a1t, c1t, a2t, r2t, expert_u[layer].astype(BF16),
                     expert_v[layer].T.astype(BF16))

        x2 = _final(x1, y, p[layer].reshape(b * s, -1), norm_ple[layer], w_ple_gate[layer].astype(BF16),
                    w_ple[layer].astype(BF16), norm_final, last=layer == depth - 1)
    return x2.reshape(b, s, d)
```
